```python
import math
import jax
import jax.numpy as jnp
from jax import lax
import numpy as np

D_MODEL = 1024
BATCH = 8
SEQ = 2048
DEPTH = 4
DEC_BATCH = 32
DEC_SEQ = 64
PAST_LEN = 1024

CHUNK = 64
N_MIXERS = 2
N_SSD_LAYERS = (DEPTH + 1) // 2
N_SWA_LAYERS = DEPTH // 2
RMS_EPS = 1e-6

SSD_EXPAND = 2
D_INNER = SSD_EXPAND * D_MODEL
SSD_HEAD_DIM = 64
SSD_HEADS = D_INNER // SSD_HEAD_DIM
SSD_GROUPS = 4
SSD_HEADS_PER_GROUP = SSD_HEADS // SSD_GROUPS
SSD_STATE = 128
CONV_WIDTH = 4
CONV_DIM = D_INNER + 2 * SSD_GROUPS * SSD_STATE
SSD_IN_DIM = D_INNER + CONV_DIM + SSD_HEADS

WINDOW = 128
WIN_CHUNKS = WINDOW // CHUNK
ATT_HEAD_DIM = 64
N_Q_HEADS = 16
N_KV_HEADS = 2
Q_PER_KV = N_Q_HEADS // N_KV_HEADS
QKV_DIM = (N_Q_HEADS + 2 * N_KV_HEADS) * ATT_HEAD_DIM
ATT_SCALE = ATT_HEAD_DIM ** -0.5

N_EXPERT_GROUPS = 8
EXPERTS_PER_GROUP = 8
N_EXPERTS = N_EXPERT_GROUPS * EXPERTS_PER_GROUP
TOP_K_GROUP = 1
TOP_K_EXPERT = 2
D_FF_EXPERT = 512
MOE_BLOCK = 128

kernel_name = "hybrid_ssd_swa_hmoe_stream_step"


def rms_norm(x, w):
    xf = x.astype(jnp.float32)
    xf = xf * lax.rsqrt(jnp.mean(xf * xf, axis=-1, keepdims=True) + RMS_EPS)
    return (xf * w.astype(jnp.float32)).astype(x.dtype)


def causal_depthwise_conv(u, buf, w, b):
    L = u.shape[1]
    up = jnp.concatenate([buf.astype(u.dtype), u], axis=1)
    y = b + sum(up[:, k:k + L] * w[k] for k in range(CONV_WIDTH))
    return y, up[:, up.shape[1] - (CONV_WIDTH - 1):]


def ssd_chunked_scan(x, dt, a, bm, cm, h0):
    b, L = x.shape[:2]
    cl = min(CHUNK, L)
    nc = L // cl

    def chunks(t):
        return t.reshape((b, nc, cl) + t.shape[2:])

    x, dt, bm, cm = chunks(x), chunks(dt), chunks(bm), chunks(cm)
    acs = jnp.cumsum(dt * a, axis=2)
    diff = acs[:, :, :, None] - acs[:, :, None, :]
    causal = jnp.tril(jnp.ones((cl, cl), dtype=bool))
    decay = jnp.exp(jnp.where(causal[:, :, None, None], diff, -jnp.inf))
    cb = jnp.einsum('bclgn,bcsgn->bclsg', cm, bm)
    mix = cb[..., None] * decay * dt[:, :, None]
    y_diag = jnp.einsum('bclsgh,bcsghp->bclghp', mix, x)
    decay_to_end = jnp.exp(acs[:, :, -1:] - acs)
    chunk_states = jnp.einsum('bclgn,bclgh,bclghp->bcghpn', bm, decay_to_end * dt, x)
    chunk_decay = jnp.exp(acs[:, :, -1])

    def step(h, inp):
        dec, st = inp
        return h * dec[..., None, None] + st, h

    h_final, h_prev = lax.scan(step, h0, (jnp.moveaxis(chunk_decay, 1, 0),
                                          jnp.moveaxis(chunk_states, 1, 0)))
    h_prev = jnp.moveaxis(h_prev, 0, 1)
    y_off = jnp.einsum('bclgn,bcghpn->bclghp', cm, h_prev) * jnp.exp(acs)[..., None]
    y = (y_diag + y_off).reshape(b, L, SSD_GROUPS, SSD_HEADS_PER_GROUP, SSD_HEAD_DIM)
    return y, h_final


def ssd_mixer(u, conv_buf, h0, w_in, conv_w, conv_b, dt_bias, a_log, d_skip, norm_w, w_out):
    f32 = jnp.float32
    b, L, _ = u.shape
    G, Hg, P, N = SSD_GROUPS, SSD_HEADS_PER_GROUP, SSD_HEAD_DIM, SSD_STATE
    proj = u @ w_in
    z, xbc, dt_raw = jnp.split(proj, [D_INNER, D_INNER + CONV_DIM], axis=-1)
    xbc, new_buf = causal_depthwise_conv(xbc, conv_buf, conv_w, conv_b)
    xbc = jax.nn.silu(xbc).astype(f32)
    xs, bm, cm = jnp.split(xbc, [D_INNER, D_INNER + G * N], axis=-1)
    xs = xs.reshape(b, L, G, Hg, P)
    bm = bm.reshape(b, L, G, N)
    cm = cm.reshape(b, L, G, N)
    dt = jax.nn.softplus(dt_raw.astype(f32) + dt_bias.astype(f32)).reshape(b, L, G, Hg)
    a = -jnp.exp(a_log.astype(f32)).reshape(G, Hg)
    y, h_final = ssd_chunked_scan(xs, dt, a, bm, cm, h0.astype(f32).reshape(b, G, Hg, P, N))
    y = y + d_skip.astype(f32).reshape(G, Hg, 1) * xs
    gsz = D_INNER // G
    g = y.reshape(b, L, G, gsz) * jax.nn.silu(z.astype(f32)).reshape(b, L, G, gsz)
    g = g * lax.rsqrt(jnp.mean(g * g, axis=-1, keepdims=True) + RMS_EPS)
    g = g.reshape(b, L, D_INNER) * norm_w.astype(f32)
    out = g.astype(u.dtype) @ w_out
    return out, new_buf, h_final.reshape(b, SSD_HEADS, P, N).astype(u.dtype)


def project_qkv(u, w_qkv, b_qkv):
    b, L, _ = u.shape
    qkv = u @ w_qkv + b_qkv
    q, k, v = jnp.split(qkv, [N_Q_HEADS * ATT_HEAD_DIM, (N_Q_HEADS + N_KV_HEADS) * ATT_HEAD_DIM], axis=-1)
    return (q.reshape(b, L, N_KV_HEADS, Q_PER_KV, ATT_HEAD_DIM),
            k.reshape(b, L, N_KV_HEADS, ATT_HEAD_DIM),
            v.reshape(b, L, N_KV_HEADS, ATT_HEAD_DIM))


def sink_softmax(s, sink):
    m = jnp.maximum(jnp.max(s, axis=-1, keepdims=True), sink)
    p = jnp.exp(s - m)
    return p / (jnp.sum(p, axis=-1, keepdims=True) + jnp.exp(sink - m))


def swa_prompt(u, w_qkv, b_qkv, sinks, w_o, b_o):
    b, L, _ = u.shape
    nc = L // CHUNK
    band_len = (WIN_CHUNKS + 1) * CHUNK
    q, k, v = project_qkv(u, w_qkv, b_qkv)

    def band(t):
        tp = jnp.pad(t, ((0, 0), (WIN_CHUNKS * CHUNK, 0), (0, 0), (0, 0)))
        tp = tp.reshape(b, nc + WIN_CHUNKS, CHUNK, N_KV_HEADS, ATT_HEAD_DIM)
        return jnp.concatenate([tp[:, j:j + nc] for j in range(WIN_CHUNKS + 1)], axis=2)

    kb, vb = band(k), band(v)
    qc = q.reshape(b, nc, CHUNK, N_KV_HEADS, Q_PER_KV, ATT_HEAD_DIM)
    s = jnp.einsum('bcqkgd,bcskd->bckgqs', qc, kb, preferred_element_type=jnp.float32) * ATT_SCALE
    key_pos = (jnp.arange(nc)[:, None] - WIN_CHUNKS) * CHUNK + jnp.arange(band_len)[None, :]
    s = jnp.where((key_pos >= 0)[None, :, None, None, None, :], s, -jnp.inf)
    sink = sinks.astype(jnp.float32).reshape(1, 1, N_KV_HEADS, Q_PER_KV, 1, 1)
    p = sink_softmax(s, sink).astype(vb.dtype)
    o = jnp.einsum('bckgqs,bcskd->bcqkgd', p, vb).reshape(b, L, N_Q_HEADS * ATT_HEAD_DIM)
    keep = min(WINDOW, L)
    return o @ w_o + b_o, k[:, L - keep:], v[:, L - keep:]


def swa_sample(u, k_cache, v_cache, w_qkv, b_qkv, sinks, w_o, b_o):
    b, L, _ = u.shape
    q, k, v = project_qkv(u, w_qkv, b_qkv)
    kf = jnp.concatenate([k_cache.astype(k.dtype), k], axis=1)
    vf = jnp.concatenate([v_cache.astype(v.dtype), v], axis=1)
    s = jnp.einsum('bqkgd,bskd->bkgqs', q, kf, preferred_element_type=jnp.float32) * ATT_SCALE
    sink = sinks.astype(jnp.float32).reshape(1, N_KV_HEADS, Q_PER_KV, 1, 1)
    p = sink_softmax(s, sink).astype(vf.dtype)
    o = jnp.einsum('bkgqs,bskd->bqkgd', p, vf).reshape(b, L, N_Q_HEADS * ATT_HEAD_DIM)
    keep = k_cache.shape[1]
    total = kf.shape[1]
    return o @ w_o + b_o, kf[:, total - keep:], vf[:, total - keep:]


def grouped_expert_ffn(xf, eidx, gates, w_gate, w_up, w_down):
    T, D = xf.shape
    K = eidx.shape[1]
    A = T * K
    flat_e = eidx.reshape(-1).astype(jnp.int32)
    order = jnp.argsort(flat_e)
    sorted_e = flat_e[order]
    counts = jnp.bincount(flat_e, length=N_EXPERTS)
    padded = (counts + MOE_BLOCK - 1) // MOE_BLOCK * MOE_BLOCK
    ends = jnp.cumsum(padded)
    pad_starts = ends - padded
    starts = jnp.cumsum(counts) - counts
    rank = jnp.arange(A, dtype=jnp.int32) - starts[sorted_e]
    dest = pad_starts[sorted_e] + rank
    n_blocks = -(-A // MOE_BLOCK) + N_EXPERTS
    n_slots = n_blocks * MOE_BLOCK
    token_of_slot = jnp.full((n_slots,), T, jnp.int32).at[dest].set((order // K).astype(jnp.int32))
    gate_of_slot = jnp.zeros((n_slots,), gates.dtype).at[dest].set(gates.reshape(-1)[order])
    block_expert = jnp.minimum(
        jnp.searchsorted(ends, jnp.arange(n_blocks) * MOE_BLOCK, side='right'), N_EXPERTS - 1)
    xpad = jnp.concatenate([xf, jnp.zeros((1, D), xf.dtype)], axis=0)
    xs = xpad[token_of_slot].reshape(n_blocks, MOE_BLOCK, D)

    def expert_block(args):
        xb, e = args
        hdn = jax.nn.silu(xb @ w_gate[e]) * (xb @ w_up[e])
        return hdn @ w_down[e]

    ys = lax.map(expert_block, (xs, block_expert)).reshape(n_slots, D)
    ys = ys * gate_of_slot[:, None]
    return jnp.zeros((T + 1, D), ys.dtype).at[token_of_slot].add(ys)[:T]


def hier_moe(h, w_grp, b_grp, w_exp, b_exp, w_gate, w_up, w_down):
    b, L, D = h.shape
    xf = h.reshape(b * L, D)
    T = xf.shape[0]
    grp_logits = (xf @ w_grp + b_grp).astype(jnp.float32)
    grp_prob = jax.nn.softmax(grp_logits, axis=-1)
    _, g_idx = lax.top_k(grp_logits, TOP_K_GROUP)
    p_grp = jnp.take_along_axis(grp_prob, g_idx, axis=-1)
    exp_logits = (xf @ w_exp + b_exp).astype(jnp.float32).reshape(T, N_EXPERT_GROUPS, EXPERTS_PER_GROUP)
    sel_logits = exp_logits[jnp.arange(T), g_idx[:, 0]]
    top_v, top_i = lax.top_k(sel_logits, TOP_K_EXPERT)
    gates = (p_grp * jax.nn.softmax(top_v, axis=-1)).astype(h.dtype)
    eidx = g_idx * EXPERTS_PER_GROUP + top_i
    out = grouped_expert_ffn(xf, eidx, gates, w_gate, w_up, w_down)
    return out.reshape(b, L, D)


def setup_inputs(seed: int = 0) -> dict:
    key = jax.random.key(seed)
    keys = iter(jax.random.split(key, 64))

    def normal(shape, scale):
        return jax.random.normal(next(keys), shape, jnp.float32) * scale

    def uniform(shape, lo, hi):
        return jax.random.uniform(next(keys), shape, jnp.float32, lo, hi)

    out_scale = (2 * DEPTH) ** -0.5
    swa_cache_len = min(WINDOW, PAST_LEN)
    dt0 = jnp.exp(uniform((N_SSD_LAYERS, SSD_HEADS), math.log(1e-3), math.log(1e-1)))
    return {
        "x_prompt": normal((BATCH, SEQ, D_MODEL), 1.0),
        "x_sample": normal((DEC_BATCH, DEC_SEQ, D_MODEL), 1.0),
        "cache_ssd_conv": normal((N_SSD_LAYERS, DEC_BATCH, CONV_WIDTH - 1, CONV_DIM), 1.0),
        "state_ssd": normal((N_SSD_LAYERS, DEC_BATCH, SSD_HEADS, SSD_HEAD_DIM, SSD_STATE), 0.1),
        "cache_swa_k": normal((N_SWA_LAYERS, DEC_BATCH, swa_cache_len, N_KV_HEADS, ATT_HEAD_DIM), 1.0),
        "cache_swa_v": normal((N_SWA_LAYERS, DEC_BATCH, swa_cache_len, N_KV_HEADS, ATT_HEAD_DIM), 1.0),
        "norm_mix_w": 1.0 + normal((DEPTH, D_MODEL), 0.05),
        "norm_ffn_w": 1.0 + normal((DEPTH, D_MODEL), 0.05),
        "norm_final_w": 1.0 + normal((D_MODEL,), 0.05),
        "ssd_in_w": normal((N_SSD_LAYERS, D_MODEL, SSD_IN_DIM), D_MODEL ** -0.5),
        "ssd_conv_w": normal((N_SSD_LAYERS, CONV_WIDTH, CONV_DIM), CONV_WIDTH ** -0.5),
        "ssd_conv_b": normal((N_SSD_LAYERS, CONV_DIM), 0.02),
        "ssd_dt_bias": dt0 + jnp.log(-jnp.expm1(-dt0)),
        "ssd_a_log": jnp.log(uniform((N_SSD_LAYERS, SSD_HEADS), 1.0, 16.0)),
        "ssd_d": 1.0 + normal((N_SSD_LAYERS, SSD_HEADS), 0.1),
        "ssd_norm_w": 1.0 + normal((N_SSD_LAYERS, D_INNER), 0.05),
        "ssd_out_w": normal((N_SSD_LAYERS, D_INNER, D_MODEL), D_INNER ** -0.5 * out_scale),
        "swa_qkv_w": normal((N_SWA_LAYERS, D_MODEL, QKV_DIM), D_MODEL ** -0.5),
        "swa_qkv_b": normal((N_SWA_LAYERS, QKV_DIM), 0.02),
        "swa_sinks": normal((N_SWA_LAYERS, N_Q_HEADS), 0.5),
        "swa_o_w": normal((N_SWA_LAYERS, N_Q_HEADS * ATT_HEAD_DIM, D_MODEL), (N_Q_HEADS * ATT_HEAD_DIM) ** -0.5 * out_scale),
        "swa_o_b": normal((N_SWA_LAYERS, D_MODEL), 0.02),
        "moe_group_w": normal((DEPTH, D_MODEL, N_EXPERT_GROUPS), D_MODEL ** -0.5),
        "moe_group_b": normal((DEPTH, N_EXPERT_GROUPS), 0.01),
        "moe_expert_w": normal((DEPTH, D_MODEL, N_EXPERTS), D_MODEL ** -0.5),
        "moe_expert_b": normal((DEPTH, N_EXPERTS), 0.01),
        "moe_w_gate": normal((DEPTH, N_EXPERTS, D_MODEL, D_FF_EXPERT), D_MODEL ** -0.5),
        "moe_w_up": normal((DEPTH, N_EXPERTS, D_MODEL, D_FF_EXPERT), D_MODEL ** -0.5),
        "moe_w_down": normal((DEPTH, N_EXPERTS, D_FF_EXPERT, D_MODEL), D_FF_EXPERT ** -0.5 * out_scale),
    }


def reference(x_prompt, x_sample, cache_ssd_conv, state_ssd, cache_swa_k, cache_swa_v,
              norm_mix_w, norm_ffn_w, norm_final_w,
              ssd_in_w, ssd_conv_w, ssd_conv_b, ssd_dt_bias, ssd_a_log, ssd_d, ssd_norm_w, ssd_out_w,
              swa_qkv_w, swa_qkv_b, swa_sinks, swa_o_w, swa_o_b,
              moe_group_w, moe_group_b, moe_expert_w, moe_expert_b, moe_w_gate, moe_w_up, moe_w_down):
    hp, hs = x_prompt, x_sample
    bp = hp.shape[0]
    p_conv, p_ssm, p_k, p_v = [], [], [], []
    s_conv, s_ssm, s_k, s_v = [], [], [], []
    for i in range(DEPTH):
        j = i // N_MIXERS
        up = rms_norm(hp, norm_mix_w[i])
        us = rms_norm(hs, norm_mix_w[i])
        if i % N_MIXERS == 0:
            w = (ssd_in_w[j], ssd_conv_w[j], ssd_conv_b[j], ssd_dt_bias[j], ssd_a_log[j],
                 ssd_d[j], ssd_norm_w[j], ssd_out_w[j])
            zero_buf = jnp.zeros((bp, CONV_WIDTH - 1, CONV_DIM), hp.dtype)
            zero_h = jnp.zeros((bp, SSD_HEADS, SSD_HEAD_DIM, SSD_STATE), hp.dtype)
            mp, cbp, hfp = ssd_mixer(up, zero_buf, zero_h, *w)
            ms, cbs, hfs = ssd_mixer(us, cache_ssd_conv[j], state_ssd[j], *w)
            p_conv.append(cbp)
            p_ssm.append(hfp)
            s_conv.append(cbs)
            s_ssm.append(hfs)
        else:
            w = (swa_qkv_w[j], swa_qkv_b[j], swa_sinks[j], swa_o_w[j], swa_o_b[j])
            mp, kp, vp = swa_prompt(up, *w)
            ms, ks, vs = swa_sample(us, cache_swa_k[j], cache_swa_v[j], *w)
            p_k.append(kp)
            p_v.append(vp)
            s_k.append(ks)
            s_v.append(vs)
        hp = hp + mp
        hs = hs + ms
        wm = (moe_group_w[i], moe_group_b[i], moe_expert_w[i], moe_expert_b[i],
              moe_w_gate[i], moe_w_up[i], moe_w_down[i])
        hp = hp + hier_moe(rms_norm(hp, norm_ffn_w[i]), *wm)
        hs = hs + hier_moe(rms_norm(hs, norm_ffn_w[i]), *wm)
    y_prompt = rms_norm(hp, norm_final_w)
    y_sample = rms_norm(hs, norm_final_w)
    prompt_ssd_conv = jnp.stack(p_conv)
    prompt_ssd_state = jnp.stack(p_ssm)
    prompt_swa_k = jnp.stack(p_k)
    prompt_swa_v = jnp.stack(p_v)
    sample_ssd_conv = jnp.stack(s_conv)
    sample_ssd_state = jnp.stack(s_ssm)
    sample_swa_k = jnp.stack(s_k)
    sample_swa_v = jnp.stack(s_v)
    return (y_prompt, y_sample, prompt_ssd_conv, prompt_ssd_state, prompt_swa_k, prompt_swa_v,
            sample_ssd_conv, sample_ssd_state, sample_swa_k, sample_swa_v)
```

```python
import functools

import jax
import jax.numpy as jnp
from jax import lax
from jax.experimental import pallas as pl
from jax.experimental.pallas import tpu as pltpu

F32 = jnp.float32
BF16 = jnp.bfloat16

D_MODEL = 1024
BATCH = 8
SEQ = 2048
DEPTH = 4
DEC_BATCH = 32
DEC_SEQ = 64
CHUNK = 64
RMS_EPS = 1e-6

D_INNER = 2048
SSD_HEAD_DIM = 64
SSD_HEADS = 32
SSD_GROUPS = 4
SSD_STATE = 128
CONV_WIDTH = 4
CONV_DIM = D_INNER + 2 * SSD_GROUPS * SSD_STATE
GROUP_WIDTH = D_INNER // SSD_GROUPS

WINDOW = 128
ATT_HEAD_DIM = 64
N_Q_HEADS = 16
N_KV_HEADS = 2
Q_PER_KV = N_Q_HEADS // N_KV_HEADS
Q_DIM = N_Q_HEADS * ATT_HEAD_DIM
KV_DIM = N_KV_HEADS * ATT_HEAD_DIM
QKV_DIM = Q_DIM + 2 * KV_DIM
ATT_SCALE = ATT_HEAD_DIM ** -0.5

N_EXPERT_GROUPS = 8
EXPERTS_PER_GROUP = 8
N_EXPERTS = 64
TOP_K = 2
D_FF_EXPERT = 512
MOE_BLOCK = 128

LANES = 128
T_PROMPT = BATCH * SEQ
T_SAMPLE = DEC_BATCH * DEC_SEQ
T_ALL = T_PROMPT + T_SAMPLE
PROMPT_CHUNKS = SEQ // CHUNK
N_PROMPT_STEPS = BATCH * PROMPT_CHUNKS
N_STEPS = N_PROMPT_STEPS + DEC_BATCH
N_STREAMS = BATCH + DEC_BATCH
VMEM_LIMIT = 56 * 1024 * 1024


def _params(sem):
    return pltpu.CompilerParams(dimension_semantics=sem, vmem_limit_bytes=VMEM_LIMIT)


def _silu(x):
    return x * jax.nn.sigmoid(x)


def _norm_matmul_kernel(x_ref, nw_ref, w_ref, b_ref, o_ref, xn_ref):
    @pl.when(pl.program_id(1) == 0)
    def _():
        x = x_ref[...]
        ms = jnp.mean(x * x, axis=-1, keepdims=True)
        xn_ref[...] = ((x * lax.rsqrt(ms + RMS_EPS)) * nw_ref[...]).astype(BF16)

    o_ref[...] = jnp.dot(xn_ref[...], w_ref[...], preferred_element_type=F32) + b_ref[...]


def norm_matmul(x, nw, w, b, *, tm=512, tn=None):
    t, k = x.shape
    n = w.shape[1]
    tn = n if tn is None else tn
    return pl.pallas_call(
        _norm_matmul_kernel,
        grid=(t // tm, n // tn),
        in_specs=[
            pl.BlockSpec((tm, k), lambda i, j: (i, 0)),
            pl.BlockSpec((1, k), lambda i, j: (0, 0)),
            pl.BlockSpec((k, tn), lambda i, j: (0, j)),
            pl.BlockSpec((1, tn), lambda i, j: (0, j)),
        ],
        out_specs=pl.BlockSpec((tm, tn), lambda i, j: (i, j)),
        out_shape=jax.ShapeDtypeStruct((t, n), F32),
        scratch_shapes=[pltpu.VMEM((tm, k), BF16)],
        compiler_params=_params(("parallel", "arbitrary")),
        name="norm_matmul",
    )(x, nw.reshape(1, k), w, b.reshape(1, n))


def _matmul_res_kernel(a_ref, w_ref, b_ref, r_ref, o_ref):
    o_ref[...] = r_ref[...] + (
        jnp.dot(a_ref[...], w_ref[...], preferred_element_type=F32) + b_ref[...])


def matmul_residual(a, w, b, res, *, tm=512):
    t, k = a.shape
    n = w.shape[1]
    return pl.pallas_call(
        _matmul_res_kernel,
        grid=(t // tm,),
        in_specs=[
            pl.BlockSpec((tm, k), lambda i: (i, 0)),
            pl.BlockSpec((k, n), lambda i: (0, 0)),
            pl.BlockSpec((1, n), lambda i: (0, 0)),
            pl.BlockSpec((tm, n), lambda i: (i, 0)),
        ],
        out_specs=pl.BlockSpec((tm, n), lambda i: (i, 0)),
        out_shape=jax.ShapeDtypeStruct((t, n), F32),
        compiler_params=_params(("parallel",)),
        name="matmul_residual",
    )(a, w, b.reshape(1, n), res)


def _rms_kernel(x_ref, nw_ref, o_ref):
    x = x_ref[...]
    ms = jnp.mean(x * x, axis=-1, keepdims=True)
    o_ref[...] = (x * lax.rsqrt(ms + RMS_EPS)) * nw_ref[...]


def rms_norm_call(x, nw, *, tm=512):
    t, k = x.shape
    return pl.pallas_call(
        _rms_kernel,
        grid=(t // tm,),
        in_specs=[pl.BlockSpec((tm, k), lambda i: (i, 0)),
                  pl.BlockSpec((1, k), lambda i: (0, 0))],
        out_specs=pl.BlockSpec((tm, k), lambda i: (i, 0)),
        out_shape=jax.ShapeDtypeStruct((t, k), F32),
        compiler_params=_params(("parallel",)),
        name="final_rms_norm",
    )(x, nw.reshape(1, k))


N_PAIRS = SSD_HEADS // 2
TAIL = CONV_WIDTH - 1
UP_OFF = 8


def _expand_heads(arr, lo):
    rows = arr.shape[0]
    tiles = []
    for j in range(N_PAIRS):
        even = jnp.broadcast_to(arr[:, j:j + 1], (rows, LANES))
        odd = jnp.broadcast_to(arr[:, N_PAIRS + j:N_PAIRS + j + 1], (rows, LANES))
        tiles.append(jnp.where(lo, even, odd))
    return jnp.concatenate(tiles, axis=1)


def _ssd_kernel(z_ref, xbc_ref, dt_ref, convw_ref, convb_ref, dtb_ref, alog_ref, dskip_ref,
                normw_ref, cache_ref, h0_ref, g_ref, tail_ref, hfin_ref, up_ref, st_ref):
    L = CHUNK
    s = pl.program_id(0)
    is_prompt = s < N_PROMPT_STEPS
    c = s % PROMPT_CHUNKS
    first = jnp.logical_or(jnp.logical_not(is_prompt), c == 0)
    last = jnp.logical_or(jnp.logical_not(is_prompt), c == PROMPT_CHUNKS - 1)

    @pl.when(jnp.logical_and(first, is_prompt))
    def _():
        up_ref[UP_OFF - TAIL:UP_OFF, :] = jnp.zeros((TAIL, CONV_DIM), F32)
        st_ref[...] = jnp.zeros_like(st_ref)

    @pl.when(jnp.logical_not(is_prompt))
    def _():
        up_ref[UP_OFF - TAIL:UP_OFF, :] = cache_ref[0]
        st_ref[...] = h0_ref[0].T

    x_raw = xbc_ref[...]
    up_ref[UP_OFF:UP_OFF + L, :] = x_raw
    conv = up_ref[UP_OFF - TAIL:UP_OFF - TAIL + L, :] * convw_ref[0:1, :]
    for k in range(1, CONV_WIDTH):
        conv = conv + up_ref[UP_OFF - TAIL + k:UP_OFF - TAIL + k + L, :] * convw_ref[k:k + 1, :]
    up_ref[UP_OFF - TAIL:UP_OFF, :] = x_raw[L - TAIL:L, :]
    xbc = _silu(convb_ref[...] + conv)
    xs = xbc[:, :D_INNER]
    bm = xbc[:, D_INNER:D_INNER + SSD_GROUPS * SSD_STATE]
    cm = xbc[:, D_INNER + SSD_GROUPS * SSD_STATE:]

    dt_in = dt_ref[...] + dtb_ref[...]
    dtv = jnp.maximum(dt_in, 0.0) + jnp.log1p(jnp.exp(-jnp.abs(dt_in)))
    a = -jnp.exp(alog_ref[...])
    dta = dtv * a
    ri = lax.broadcasted_iota(jnp.int32, (L, L), 0)
    ci = lax.broadcasted_iota(jnp.int32, (L, L), 1)
    tril = (ri >= ci).astype(F32)
    acs = jnp.dot(tril, dta, preferred_element_type=F32, precision=lax.Precision.HIGHEST)
    pad = jnp.zeros((LANES - L, LANES), F32)
    dt_t = jnp.concatenate([dtv, pad], axis=0).T
    dta_t = jnp.concatenate([dta, pad], axis=0).T
    r2 = lax.broadcasted_iota(jnp.int32, (LANES, LANES), 0)
    c2 = lax.broadcasted_iota(jnp.int32, (LANES, LANES), 1)
    cum2 = jnp.logical_and(r2 <= c2 % L, r2 < L).astype(F32)
    dup2 = (r2 == c2 % L).astype(F32)
    acs_t = jnp.dot(dta_t, cum2, preferred_element_type=F32, precision=lax.Precision.HIGHEST)
    dt_t2 = jnp.dot(dt_t, dup2, preferred_element_type=F32, precision=lax.Precision.HIGHEST)
    lane_r = lax.broadcasted_iota(jnp.int32, (N_PAIRS, LANES), 1)
    row_acs = jnp.where(lane_r < L, acs_t[0:N_PAIRS, :], acs_t[N_PAIRS:2 * N_PAIRS, :])
    row_dt = jnp.where(lane_r < L, dt_t2[0:N_PAIRS, :], dt_t2[N_PAIRS:2 * N_PAIRS, :])

    lane = lax.broadcasted_iota(jnp.int32, (L, LANES), 1)
    rowi = lax.broadcasted_iota(jnp.int32, (L, LANES), 0)
    lo = lane < SSD_HEAD_DIM
    causal2 = rowi >= lane % L

    total = acs[L - 1:L, :]
    dte = jnp.exp(total - acs) * dtv
    acs_x = _expand_heads(acs, lo)
    dte_x = _expand_heads(dte, lo)
    eacs_x = jnp.exp(acs_x)
    chunk_decay_x = eacs_x[L - 1:L, :]

    xs_b = xs.astype(BF16)
    xw_b = (xs * dte_x).astype(BF16)
    cm_b = cm.astype(BF16)
    bm_b = bm.astype(BF16)
    bm_t = jnp.concatenate([bm, jnp.zeros((LANES - L, bm.shape[1]), F32)], axis=0).T
    xw_pad = jnp.concatenate([xw_b, jnp.zeros((LANES - L, D_INNER), BF16)], axis=0)

    y_tiles = []
    for g in range(SSD_GROUPS):
        cg = cm_b[:, g * SSD_STATE:(g + 1) * SSD_STATE]
        bg = bm_b[:, g * SSD_STATE:(g + 1) * SSD_STATE]
        bb = jnp.concatenate([bg, bg], axis=0)
        cb2 = lax.dot_general(cg, bb, (((1,), (1,)), ((), ())), preferred_element_type=F32)
        sl = slice(g * GROUP_WIDTH, (g + 1) * GROUP_WIDTH)
        st_prev = st_ref[:, sl]
        y_off = jnp.dot(cg, st_prev.astype(BF16), preferred_element_type=F32) * eacs_x[:, sl]
        bg_t = bm_t[g * SSD_STATE:(g + 1) * SSD_STATE, :].astype(BF16)
        st_ref[:, sl] = st_prev * chunk_decay_x[:, sl] + jnp.dot(
            bg_t, xw_pad[:, sl], preferred_element_type=F32)
        for jj in range(N_PAIRS // SSD_GROUPS):
            j = g * (N_PAIRS // SSD_GROUPS) + jj
            ps = slice(j * LANES, (j + 1) * LANES)
            diff = acs_x[:, ps] - row_acs[j:j + 1, :]
            dec = jnp.exp(jnp.where(causal2, diff, -jnp.inf))
            m = ((cb2 * dec) * row_dt[j:j + 1, :]).astype(BF16)
            xp = xs_b[:, ps]
            zero = jnp.zeros_like(xp)
            xx = jnp.concatenate([jnp.where(lo, xp, zero), jnp.where(lo, zero, xp)], axis=0)
            y_diag = jnp.dot(m, xx, preferred_element_type=F32)
            y_tiles.append(y_diag + y_off[:, jj * LANES:(jj + 1) * LANES])
    y = jnp.concatenate(y_tiles, axis=1) + dskip_ref[...] * xs

    gated = y * _silu(z_ref[...])
    outs = []
    for g in range(SSD_GROUPS):
        sl = slice(g * GROUP_WIDTH, (g + 1) * GROUP_WIDTH)
        gg = gated[:, sl]
        ms = jnp.mean(gg * gg, axis=-1, keepdims=True)
        outs.append((gg * lax.rsqrt(ms + RMS_EPS)) * normw_ref[:, sl])
    g_ref[...] = jnp.concatenate(outs, axis=1).astype(BF16)

    @pl.when(last)
    def _():
        tail_ref[0] = up_ref[UP_OFF - TAIL:UP_OFF, :]
        hfin_ref[0] = st_ref[...].T


def _sample_block(s):
    return jnp.maximum(s - N_PROMPT_STEPS, 0)


def _stream_of_step(s):
    return jnp.where(s < N_PROMPT_STEPS, s // PROMPT_CHUNKS, s - N_PROMPT_STEPS + BATCH)


def ssd_core(z, xbc, dt, conv_w, conv_b, dt_bias, a_log, d_skip_x, norm_w, cache, h0):
    full = lambda r, c: pl.BlockSpec((r, c), lambda s: (0, 0))
    return pl.pallas_call(
        _ssd_kernel,
        grid=(N_STEPS,),
        in_specs=[
            pl.BlockSpec((CHUNK, D_INNER), lambda s: (s, 0)),
            pl.BlockSpec((CHUNK, CONV_DIM), lambda s: (s, 0)),
            pl.BlockSpec((CHUNK, LANES), lambda s: (s, 0)),
            full(CONV_WIDTH, CONV_DIM),
            full(1, CONV_DIM),
            full(1, LANES),
            full(1, LANES),
            full(1, D_INNER),
            full(1, D_INNER),
            pl.BlockSpec((1, TAIL, CONV_DIM), lambda s: (_sample_block(s), 0, 0)),
            pl.BlockSpec((1, D_INNER, SSD_STATE), lambda s: (_sample_block(s), 0, 0)),
        ],
        out_specs=[
            pl.BlockSpec((CHUNK, D_INNER), lambda s: (s, 0)),
            pl.BlockSpec((1, TAIL, CONV_DIM), lambda s: (_stream_of_step(s), 0, 0)),
            pl.BlockSpec((1, D_INNER, SSD_STATE), lambda s: (_stream_of_step(s), 0, 0)),
        ],
        out_shape=[
            jax.ShapeDtypeStruct((T_ALL, D_INNER), BF16),
            jax.ShapeDtypeStruct((N_STREAMS, TAIL, CONV_DIM), F32),
            jax.ShapeDtypeStruct((N_STREAMS, D_INNER, SSD_STATE), F32),
        ],
        scratch_shapes=[
            pltpu.VMEM((UP_OFF + CHUNK, CONV_DIM), F32),
            pltpu.VMEM((SSD_STATE, D_INNER), F32),
        ],
        compiler_params=_params(("arbitrary",)),
        name="ssd_core",
    )(z, xbc, dt, conv_w, conv_b.reshape(1, CONV_DIM), dt_bias, a_log, d_skip_x,
      norm_w.reshape(1, D_INNER), cache, h0)


def _compact_heads(v):
    even = v[..., 0::2]
    odd = v[..., 1::2]
    padw = [(0, 0)] * (v.ndim - 1) + [(0, LANES - SSD_HEADS)]
    return jnp.pad(jnp.concatenate([even, odd], axis=-1), padw)


def _swa_kernel(sink_ref, q_ref, k2_ref, k1_ref, k0_ref, v2_ref, v1_ref, v0_ref,
                ck_ref, cv_ref, o_ref):
    L = CHUNK
    s = pl.program_id(0)
    is_prompt = s < N_PROMPT_STEPS
    c = s % PROMPT_CHUNKS
    valid2 = jnp.logical_or(jnp.logical_not(is_prompt), c >= 2)
    valid1 = jnp.logical_or(jnp.logical_not(is_prompt), c >= 1)

    ka = jnp.where(is_prompt, k2_ref[...], ck_ref[0, 0:L, :])
    kb = jnp.where(is_prompt, k1_ref[...], ck_ref[0, L:2 * L, :])
    va = jnp.where(is_prompt, v2_ref[...], cv_ref[0, 0:L, :])
    vb = jnp.where(is_prompt, v1_ref[...], cv_ref[0, L:2 * L, :])
    kband = jnp.concatenate([ka, kb, k0_ref[...]], axis=0).astype(BF16)
    vband = jnp.concatenate([va, vb, v0_ref[...]], axis=0).astype(BF16)
    q = q_ref[...].astype(BF16)

    col = lax.broadcasted_iota(jnp.int32, (Q_PER_KV * L, 3 * L), 1)
    key_ok = jnp.logical_and(jnp.logical_or(col >= L, valid2),
                             jnp.logical_or(col >= 2 * L, valid1))

    tiles = []
    for kv in range(N_KV_HEADS):
        hs = slice(kv * ATT_HEAD_DIM, (kv + 1) * ATT_HEAD_DIM)
        qs = jnp.concatenate(
            [q[:, (kv * Q_PER_KV + g) * ATT_HEAD_DIM:(kv * Q_PER_KV + g + 1) * ATT_HEAD_DIM]
             for g in range(Q_PER_KV)], axis=0)
        sc = lax.dot_general(qs, kband[:, hs], (((1,), (1,)), ((), ())),
                             preferred_element_type=F32) * ATT_SCALE
        sc = jnp.where(key_ok, sc, -jnp.inf)
        sink = jnp.concatenate(
            [jnp.full((L, 1), sink_ref[kv * Q_PER_KV + g], F32) for g in range(Q_PER_KV)], axis=0)
        m = jnp.maximum(jnp.max(sc, axis=-1, keepdims=True), sink)
        p = jnp.exp(sc - m)
        p = p / (jnp.sum(p, axis=-1, keepdims=True) + jnp.exp(sink - m))
        o = jnp.dot(p.astype(BF16), vband[:, hs], preferred_element_type=F32)
        for g in range(0, Q_PER_KV, 2):
            tiles.append(jnp.concatenate([o[g * L:(g + 1) * L, :], o[(g + 1) * L:(g + 2) * L, :]],
                                         axis=1))
    o_ref[...] = jnp.concatenate(tiles, axis=1).astype(BF16)


def swa_core(qkv, sinks, cache_k, cache_v):
    kcol = Q_DIM // KV_DIM
    vcol = kcol + 1

    def kspec(n, colblock):
        return pl.BlockSpec(
            (CHUNK, KV_DIM),
            lambda s, sink_ref: (jnp.where(s < N_PROMPT_STEPS, jnp.maximum(s - n, 0), s), colblock))

    cache_spec = pl.BlockSpec((1, WINDOW, KV_DIM), lambda s, sink_ref: (_sample_block(s), 0, 0))
    grid_spec = pltpu.PrefetchScalarGridSpec(
        num_scalar_prefetch=1,
        grid=(N_STEPS,),
        in_specs=[
            pl.BlockSpec((CHUNK, Q_DIM), lambda s, sink_ref: (s, 0)),
            kspec(2, kcol), kspec(1, kcol), kspec(0, kcol),
            kspec(2, vcol), kspec(1, vcol), kspec(0, vcol),
            cache_spec, cache_spec,
        ],
        out_specs=pl.BlockSpec((CHUNK, Q_DIM), lambda s, sink_ref: (s, 0)),
    )
    return pl.pallas_call(
        _swa_kernel,
        grid_spec=grid_spec,
        out_shape=jax.ShapeDtypeStruct((T_ALL, Q_DIM), BF16),
        compiler_params=_params(("parallel",)),
        name="swa_core",
    )(sinks, qkv, qkv, qkv, qkv, qkv, qkv, qkv, cache_k, cache_v)


ROUTE_E0, ROUTE_E1, ROUTE_G0, ROUTE_G1 = 0, 1, 2, 3


def _router_kernel(x_ref, nw_ref, wr_ref, br_ref, xn_ref, route_ref):
    x = x_ref[...]
    ms = jnp.mean(x * x, axis=-1, keepdims=True)
    xn = (x * lax.rsqrt(ms + RMS_EPS)) * nw_ref[...]
    xn_ref[...] = xn
    lg = jnp.dot(xn, wr_ref[...], preferred_element_type=F32,
                 precision=lax.Precision.HIGHEST) + br_ref[...]
    tm = lg.shape[0]
    lane = lax.broadcasted_iota(jnp.int32, (tm, LANES), 1)
    lanef = lane.astype(F32)
    big = jnp.float32(LANES)
    neg = -jnp.inf

    is_grp = jnp.logical_and(lane >= N_EXPERTS, lane < N_EXPERTS + N_EXPERT_GROUPS)
    gl = jnp.where(is_grp, lg, neg)
    gmax = jnp.max(gl, axis=-1, keepdims=True)
    p_grp = 1.0 / jnp.sum(jnp.exp(gl - gmax), axis=-1, keepdims=True)
    gidx = jnp.min(jnp.where(gl == gmax, lanef, big), axis=-1, keepdims=True) - N_EXPERTS

    in_grp = jnp.logical_and(lane < N_EXPERTS,
                             (lane // EXPERTS_PER_GROUP).astype(F32) == gidx)
    el = jnp.where(in_grp, lg, neg)
    v1 = jnp.max(el, axis=-1, keepdims=True)
    i1 = jnp.min(jnp.where(el == v1, lanef, big), axis=-1, keepdims=True)
    el2 = jnp.where(lanef == i1, neg, el)
    v2 = jnp.max(el2, axis=-1, keepdims=True)
    i2 = jnp.min(jnp.where(el2 == v2, lanef, big), axis=-1, keepdims=True)
    e2 = jnp.exp(v2 - v1)
    den = 1.0 + e2
    g1 = p_grp * (1.0 / den)
    g2 = p_grp * (e2 / den)
    route = jnp.where(lane == ROUTE_E0, i1,
                      jnp.where(lane == ROUTE_E1, i2,
                                jnp.where(lane == ROUTE_G0, g1,
                                          jnp.where(lane == ROUTE_G1, g2, 0.0))))
    route_ref[...] = route


def moe_router(h, nw, wr, br, *, tm=256):
    t, k = h.shape
    return pl.pallas_call(
        _router_kernel,
        grid=(t // tm,),
        in_specs=[
            pl.BlockSpec((tm, k), lambda i: (i, 0)),
            pl.BlockSpec((1, k), lambda i: (0, 0)),
            pl.BlockSpec((k, LANES), lambda i: (0, 0)),
            pl.BlockSpec((1, LANES), lambda i: (0, 0)),
        ],
        out_specs=[pl.BlockSpec((tm, k), lambda i: (i, 0)),
                   pl.BlockSpec((tm, LANES), lambda i: (i, 0))],
        out_shape=[jax.ShapeDtypeStruct((t, k), F32),
                   jax.ShapeDtypeStruct((t, LANES), F32)],
        compiler_params=_params(("parallel",)),
        name="moe_router",
    )(h, nw.reshape(1, k), wr, br)


N_PAIRS_ALL = T_ALL * TOP_K
N_BLOCKS = N_PAIRS_ALL // MOE_BLOCK + N_EXPERTS
N_SLOTS = N_BLOCKS * MOE_BLOCK


def _row_gather(src_hbm, idx_ref, dst, sem, n):
    def body(r, carry):
        pltpu.make_async_copy(src_hbm.at[pl.ds(idx_ref[0, 0, r], 1)],
                              dst.at[pl.ds(r, 1)], sem).start()
        return carry
    lax.fori_loop(0, n, body, 0, unroll=8)


def _gather_wait(src_hbm, dst, sem, n):
    pltpu.make_async_copy(src_hbm.at[pl.ds(0, n)], dst, sem).wait()


def _expert_kernel(be_ref, nused_ref, tok_ref, tok_next_ref, x_hbm, wg_ref, wu_ref, wd_ref,
                   y_ref, xbuf, sems):
    del be_ref
    i = pl.program_id(0)
    nused = nused_ref[0]
    slot = i % 2

    @pl.when(i == 0)
    def _():
        _row_gather(x_hbm, tok_ref, xbuf.at[0], sems.at[0], MOE_BLOCK)

    @pl.when(i + 1 < nused)
    def _():
        _row_gather(x_hbm, tok_next_ref, xbuf.at[1 - slot], sems.at[1 - slot], MOE_BLOCK)

    @pl.when(jnp.logical_or(i < nused, i == 0))
    def _():
        _gather_wait(x_hbm, xbuf.at[slot], sems.at[slot], MOE_BLOCK)

    @pl.when(i < nused)
    def _():
        xb = xbuf[slot].astype(BF16)
        hg = jnp.dot(xb, wg_ref[0], preferred_element_type=F32)
        hu = jnp.dot(xb, wu_ref[0], preferred_element_type=F32)
        hdn = (_silu(hg) * hu).astype(BF16)
        y_ref[...] = jnp.dot(hdn, wd_ref[0], preferred_element_type=F32)

    @pl.when(i >= nused)
    def _():
        y_ref[...] = jnp.zeros_like(y_ref)


def moe_experts(xn, tok_of_slot, block_expert, nused, wg, wu, wd):
    tok3 = tok_of_slot.reshape(N_BLOCKS, 1, MOE_BLOCK)
    smem_blk = lambda f: pl.BlockSpec((1, 1, MOE_BLOCK), f, memory_space=pltpu.SMEM)
    grid_spec = pltpu.PrefetchScalarGridSpec(
        num_scalar_prefetch=2,
        grid=(N_BLOCKS,),
        in_specs=[
            smem_blk(lambda i, be, nu: (i, 0, 0)),
            smem_blk(lambda i, be, nu: (jnp.minimum(i + 1, N_BLOCKS - 1), 0, 0)),
            pl.BlockSpec(memory_space=pl.ANY),
            pl.BlockSpec((1, D_MODEL, D_FF_EXPERT), lambda i, be, nu: (be[i], 0, 0)),
            pl.BlockSpec((1, D_MODEL, D_FF_EXPERT), lambda i, be, nu: (be[i], 0, 0)),
            pl.BlockSpec((1, D_FF_EXPERT, D_MODEL), lambda i, be, nu: (be[i], 0, 0)),
        ],
        out_specs=pl.BlockSpec((MOE_BLOCK, D_MODEL), lambda i, be, nu: (i, 0)),
        scratch_shapes=[pltpu.VMEM((2, MOE_BLOCK, D_MODEL), F32),
                        pltpu.SemaphoreType.DMA((2,))],
    )
    return pl.pallas_call(
        _expert_kernel,
        grid_spec=grid_spec,
        out_shape=jax.ShapeDtypeStruct((N_SLOTS, D_MODEL), F32),
        compiler_params=_params(("arbitrary",)),
        name="moe_experts",
    )(block_expert, nused, tok3, tok3, xn, wg, wu, wd)


COMBINE_TOKENS = 64


def _combine_kernel(slot_ref, slot_next_ref, h_ref, route_ref, y_hbm, o_ref, ybuf, sems):
    i = pl.program_id(0)
    n = pl.num_programs(0)
    slot = i % 2
    rows = TOP_K * COMBINE_TOKENS

    @pl.when(i == 0)
    def _():
        _row_gather(y_hbm, slot_ref, ybuf.at[0], sems.at[0], rows)

    @pl.when(i + 1 < n)
    def _():
        _row_gather(y_hbm, slot_next_ref, ybuf.at[1 - slot], sems.at[1 - slot], rows)

    _gather_wait(y_hbm, ybuf.at[slot], sems.at[slot], rows)
    route = route_ref[...]
    g0 = route[:, ROUTE_G0:ROUTE_G0 + 1]
    g1 = route[:, ROUTE_G1:ROUTE_G1 + 1]
    y0 = ybuf[slot, 0:COMBINE_TOKENS, :]
    y1 = ybuf[slot, COMBINE_TOKENS:rows, :]
    o_ref[...] = h_ref[...] + (y0 * g0 + y1 * g1)


def moe_combine(h, route, ys, slot_of_pair):
    t, k = h.shape
    tq = COMBINE_TOKENS
    nsteps = t // tq
    slots3 = slot_of_pair.reshape(nsteps, tq, TOP_K).transpose(0, 2, 1).reshape(nsteps, 1, TOP_K * tq)
    smem_blk = lambda f: pl.BlockSpec((1, 1, TOP_K * tq), f, memory_space=pltpu.SMEM)
    return pl.pallas_call(
        _combine_kernel,
        grid=(nsteps,),
        in_specs=[
            smem_blk(lambda i: (i, 0, 0)),
            smem_blk(lambda i: (jnp.minimum(i + 1, nsteps - 1), 0, 0)),
            pl.BlockSpec((tq, k), lambda i: (i, 0)),
            pl.BlockSpec((tq, LANES), lambda i: (i, 0)),
            pl.BlockSpec(memory_space=pl.ANY),
        ],
        out_specs=pl.BlockSpec((tq, k), lambda i: (i, 0)),
        out_shape=jax.ShapeDtypeStruct((t, k), F32),
        scratch_shapes=[pltpu.VMEM((2, TOP_K * tq, k), F32),
                        pltpu.SemaphoreType.DMA((2,))],
        compiler_params=_params(("arbitrary",)),
        name="moe_combine",
    )(slots3, slots3, h, route, ys)


def _dispatch_plan(route):
    eidx = route[:, ROUTE_E0:ROUTE_E1 + 1].astype(jnp.int32)
    flat_e = eidx.reshape(-1)
    a = flat_e.shape[0]
    order = jnp.argsort(flat_e)
    sorted_e = flat_e[order]
    counts = jnp.bincount(flat_e, length=N_EXPERTS)
    padded = (counts + MOE_BLOCK - 1) // MOE_BLOCK * MOE_BLOCK
    ends = jnp.cumsum(padded)
    pad_starts = ends - padded
    starts = jnp.cumsum(counts) - counts
    rank = jnp.arange(a, dtype=jnp.int32) - starts[sorted_e].astype(jnp.int32)
    dest = pad_starts[sorted_e].astype(jnp.int32) + rank
    tok_of_slot = jnp.zeros((N_SLOTS,), jnp.int32).at[dest].set((order // TOP_K).astype(jnp.int32))
    slot_of_pair = jnp.zeros((a,), jnp.int32).at[order].set(dest)
    block_expert = jnp.minimum(
        jnp.searchsorted(ends, jnp.arange(N_BLOCKS) * MOE_BLOCK, side='right'),
        N_EXPERTS - 1).astype(jnp.int32)
    nused = (ends[-1] // MOE_BLOCK).astype(jnp.int32).reshape(1)
    return tok_of_slot, slot_of_pair, block_expert, nused


def hier_moe_residual(h, nw, wr, br, wg, wu, wd):
    xn, route = moe_router(h, nw, wr, br)
    tok_of_slot, slot_of_pair, block_expert, nused = _dispatch_plan(route)
    ys = moe_experts(xn, tok_of_slot, block_expert, nused, wg, wu, wd)
    return moe_combine(h, route, ys, slot_of_pair)


def kernel(x_prompt, x_sample, cache_ssd_conv, state_ssd, cache_swa_k, cache_swa_v, norm_mix_w, norm_ffn_w, norm_final_w, ssd_in_w, ssd_conv_w, ssd_conv_b, ssd_dt_bias, ssd_a_log, ssd_d, ssd_norm_w, ssd_out_w, swa_qkv_w, swa_qkv_b, swa_sinks, swa_o_w, swa_o_b, moe_group_w, moe_group_b, moe_expert_w, moe_expert_b, moe_w_gate, moe_w_up, moe_w_down):
    h = jnp.concatenate([x_prompt.reshape(T_PROMPT, D_MODEL),
                         x_sample.reshape(T_SAMPLE, D_MODEL)], axis=0)
    p_conv, p_ssm, p_k, p_v = [], [], [], []
    s_conv, s_ssm, s_k, s_v = [], [], [], []
    zero_bias = lambda n: jnp.zeros((n,), F32)

    for i in range(DEPTH):
        j = i // 2
        if i % 2 == 0:
            w_in = ssd_in_w[j]
            w_z = w_in[:, :D_INNER].astype(BF16)
            w_xbc = w_in[:, D_INNER:D_INNER + CONV_DIM].astype(BF16)
            w_dt = _compact_heads(w_in[:, D_INNER + CONV_DIM:]).astype(BF16)
            z = norm_matmul(h, norm_mix_w[i], w_z, zero_bias(D_INNER), tn=1024)
            xbc = norm_matmul(h, norm_mix_w[i], w_xbc, zero_bias(CONV_DIM), tn=1024)
            dt = norm_matmul(h, norm_mix_w[i], w_dt, zero_bias(LANES))
            g, tails, hfin = ssd_core(
                z, xbc, dt, ssd_conv_w[j], ssd_conv_b[j],
                _compact_heads(ssd_dt_bias[j]).reshape(1, LANES),
                _compact_heads(ssd_a_log[j]).reshape(1, LANES),
                jnp.repeat(ssd_d[j], SSD_HEAD_DIM).reshape(1, D_INNER),
                ssd_norm_w[j], cache_ssd_conv[j],
                state_ssd[j].reshape(DEC_BATCH, D_INNER, SSD_STATE))
            h = matmul_residual(g, ssd_out_w[j].astype(BF16), zero_bias(D_MODEL), h)
            hfin = hfin.reshape(N_STREAMS, SSD_HEADS, SSD_HEAD_DIM, SSD_STATE)
            p_conv.append(tails[:BATCH])
            s_conv.append(tails[BATCH:])
            p_ssm.append(hfin[:BATCH])
            s_ssm.append(hfin[BATCH:])
        else:
            qkv = norm_matmul(h, norm_mix_w[i], swa_qkv_w[j].astype(BF16), swa_qkv_b[j])
            ck = cache_swa_k[j].reshape(DEC_BATCH, WINDOW, KV_DIM)
            cv = cache_swa_v[j].reshape(DEC_BATCH, WINDOW, KV_DIM)
            o = swa_core(qkv, swa_sinks[j], ck, cv)
            h = matmul_residual(o, swa_o_w[j].astype(BF16), swa_o_b[j], h)
            k_all = qkv[:, Q_DIM:Q_DIM + KV_DIM]
            v_all = qkv[:, Q_DIM + KV_DIM:]
            kv_shape = (-1, WINDOW, N_KV_HEADS, ATT_HEAD_DIM)
            k_p = k_all[:T_PROMPT].reshape(BATCH, SEQ, KV_DIM)[:, SEQ - WINDOW:]
            v_p = v_all[:T_PROMPT].reshape(BATCH, SEQ, KV_DIM)[:, SEQ - WINDOW:]
            k_s = k_all[T_PROMPT:].reshape(DEC_BATCH, DEC_SEQ, KV_DIM)
            v_s = v_all[T_PROMPT:].reshape(DEC_BATCH, DEC_SEQ, KV_DIM)
            p_k.append(k_p.reshape(kv_shape))
            p_v.append(v_p.reshape(kv_shape))
            s_k.append(jnp.concatenate([ck[:, DEC_SEQ:], k_s], axis=1).reshape(kv_shape))
            s_v.append(jnp.concatenate([cv[:, DEC_SEQ:], v_s], axis=1).reshape(kv_shape))

        wr = jnp.pad(jnp.concatenate([moe_expert_w[i], moe_group_w[i]], axis=1),
                     ((0, 0), (0, LANES - N_EXPERTS - N_EXPERT_GROUPS)))
        br = jnp.pad(jnp.concatenate([moe_expert_b[i], moe_group_b[i]]),
                     (0, LANES - N_EXPERTS - N_EXPERT_GROUPS)).reshape(1, LANES)
        h = hier_moe_residual(h, norm_ffn_w[i], wr, br,
                              moe_w_gate[i].astype(BF16), moe_w_up[i].astype(BF16),
                              moe_w_down[i].astype(BF16))

    y = rms_norm_call(h, norm_final_w)
    y_prompt = y[:T_PROMPT].reshape(BATCH, SEQ, D_MODEL)
    y_sample = y[T_PROMPT:].reshape(DEC_BATCH, DEC_SEQ, D_MODEL)
    return (y_prompt, y_sample, jnp.stack(p_conv), jnp.stack(p_ssm), jnp.stack(p_k), jnp.stack(p_v),
            jnp.stack(s_conv), jnp.stack(s_ssm), jnp.stack(s_k), jnp.stack(s_v))
```
